```python
import math, functools
import jax, jax.numpy as jnp
from jax import lax
import numpy as np

D_MODEL = 1024
BATCH = 1
SEQ = 16384
DEPTH = 1
DEC_BATCH = 32
DEC_SEQ = 1
PAST_LEN = 16384
PAGE_SIZE = 128

SSM_WIDTH = D_MODEL
SSM_GROUP = 16
SSM_GROUPS = SSM_WIDTH // SSM_GROUP
SSM_STATE = 64
SSM_CHUNK = 128
N_HEADS = 8
HEAD_DIM = 64
V_DIM = 2 * HEAD_DIM
QK_WIDTH = N_HEADS * 2 * HEAD_DIM
ATTN_WIDTH = N_HEADS * V_DIM
Q_BLOCK = 128
ROPE_THETA = 10000.0
PLE_DIM = 256
LN_EPS = 1e-5
SUBLN_EPS = 1e-5
DEEPNORM_ALPHA = (2 * DEPTH) ** 0.25
DEEPNORM_BETA = (8 * DEPTH) ** -0.25
IN_COLS = 2 * SSM_WIDTH + 2 * QK_WIDTH + 2 * ATTN_WIDTH + 3 * D_MODEL

kernel_name = "hybrid_s5_diffattn_gated_decode_step"

F32 = jnp.float32


def layer_norm(x, g, b):
    xf = x.astype(F32)
    mu = xf.mean(-1, keepdims=True)
    var = jnp.square(xf - mu).mean(-1, keepdims=True)
    return ((xf - mu) * lax.rsqrt(var + LN_EPS) * g.astype(F32) + b.astype(F32)).astype(x.dtype)


def rope(x, pos):
    half = HEAD_DIM // 2
    inv = ROPE_THETA ** (-jnp.arange(half, dtype=F32) / half)
    ang = pos.astype(F32)[:, None] * inv[None, :]
    cos = jnp.cos(ang)[:, None, None, :]
    sin = jnp.sin(ang)[:, None, None, :]
    xf = x.astype(F32)
    x1, x2 = xf[..., :half], xf[..., half:]
    return jnp.concatenate([x1 * cos - x2 * sin, x2 * cos + x1 * sin], axis=-1).astype(x.dtype)


def ssm_discretise(a_re, a_im, log_dt, b_re, b_im):
    a_re = a_re.astype(F32); a_im = a_im.astype(F32)
    dt = jnp.exp(log_dt.astype(F32))[:, None]
    mag = jnp.exp(a_re * dt)
    ab_re = mag * jnp.cos(a_im * dt)
    ab_im = mag * jnp.sin(a_im * dt)
    den = a_re * a_re + a_im * a_im
    n_re = ab_re - 1.0
    co_re = (n_re * a_re + ab_im * a_im) / den
    co_im = (ab_im * a_re - n_re * a_im) / den
    b_re = b_re.astype(F32); b_im = b_im.astype(F32)
    bb_re = co_re[..., None] * b_re - co_im[..., None] * b_im
    bb_im = co_re[..., None] * b_im + co_im[..., None] * b_re
    return ab_re, ab_im, bb_re, bb_im


def _scan_combine(e1, e2):
    a1r, a1i, b1r, b1i = e1
    a2r, a2i, b2r, b2i = e2
    return (a1r * a2r - a1i * a2i,
            a1r * a2i + a1i * a2r,
            a2r * b1r - a2i * b1i + b2r,
            a2r * b1i + a2i * b1r + b2i)


def ssm_scan(u, h_re, h_im, ab_re, ab_im, bb_re, bb_im, c_re, c_im):
    bsz, L = u.shape[0], u.shape[1]
    ch = SSM_CHUNK if L % SSM_CHUNK == 0 else L
    nc = L // ch
    uc = u.reshape(bsz, nc, ch, SSM_GROUPS, SSM_GROUP).transpose(1, 0, 2, 3, 4)
    c_re = c_re.astype(F32); c_im = c_im.astype(F32)

    def body(carry, u_blk):
        hr, hi = carry
        uf = u_blk.astype(F32)
        bu_re = jnp.einsum('btgc,gpc->btgp', uf, bb_re)
        bu_im = jnp.einsum('btgc,gpc->btgp', uf, bb_im)
        a_re = jnp.broadcast_to(ab_re, bu_re.shape)
        a_im = jnp.broadcast_to(ab_im, bu_re.shape)
        pr, pi, sr, si = lax.associative_scan(_scan_combine, (a_re, a_im, bu_re, bu_im), axis=1)
        x_re = sr + pr * hr[:, None] - pi * hi[:, None]
        x_im = si + pr * hi[:, None] + pi * hr[:, None]
        y = jnp.einsum('gcp,btgp->btgc', c_re, x_re) - jnp.einsum('gcp,btgp->btgc', c_im, x_im)
        return (x_re[:, -1], x_im[:, -1]), y

    (h_re, h_im), y = lax.scan(body, (h_re.astype(F32), h_im.astype(F32)), uc)
    y = y.transpose(1, 0, 2, 3, 4).reshape(bsz, L, SSM_GROUPS, SSM_GROUP)
    return y, h_re, h_im


def diff_attn_prompt(q, k, v, lam):
    bsz, S = q.shape[0], q.shape[1]
    nb = S // Q_BLOCK
    qb = q.reshape(bsz, nb, Q_BLOCK, N_HEADS, 2, HEAD_DIM).transpose(1, 0, 2, 3, 4, 5)
    kpos = jnp.arange(S)
    scale = HEAD_DIM ** -0.5

    def block(args):
        qi, i = args
        s = jnp.einsum('bqhjd,bkhjd->bhjqk', qi, k).astype(F32) * scale
        qpos = i * Q_BLOCK + jnp.arange(Q_BLOCK)
        s = jnp.where(kpos[None, :] <= qpos[:, None], s, -jnp.inf)
        p = jax.nn.softmax(s, axis=-1)
        w = p[:, :, 0] - lam * p[:, :, 1]
        return jnp.einsum('bhqk,bkhe->bqhe', w.astype(v.dtype), v)

    o = lax.map(block, (qb, jnp.arange(nb)))
    return o.transpose(1, 0, 2, 3, 4).reshape(bsz, S, N_HEADS, V_DIM)


def diff_attn_sample(q, k, v, lam, k_past, v_past):
    L = q.shape[1]
    T = k_past.shape[1]
    scale = HEAD_DIM ** -0.5
    s_past = jnp.einsum('bqhjd,bkhjd->bhjqk', q, k_past).astype(F32) * scale
    s_new = jnp.einsum('bqhjd,bkhjd->bhjqk', q, k).astype(F32) * scale
    s_new = jnp.where(jnp.tril(jnp.ones((L, L), dtype=bool)), s_new, -jnp.inf)
    p = jax.nn.softmax(jnp.concatenate([s_past, s_new], axis=-1), axis=-1)
    w = (p[:, :, 0] - lam * p[:, :, 1]).astype(v.dtype)
    return (jnp.einsum('bhqk,bkhe->bqhe', w[..., :T], v_past)
            + jnp.einsum('bhqk,bkhe->bqhe', w[..., T:], v))


def trunk_layer(x, pe, pos, h_re, h_im, attend, lw, lam_init):
    bsz, L, _ = x.shape
    sizes = (SSM_WIDTH, SSM_WIDTH, QK_WIDTH, QK_WIDTH, ATTN_WIDTH, ATTN_WIDTH, D_MODEL, D_MODEL, D_MODEL)
    cuts = tuple(int(c) for c in np.cumsum(sizes)[:-1])
    proj = x @ lw['w_in']
    u, z_a, q, k, v, z_b, g_a, g_b, g_p = jnp.split(proj, cuts, axis=-1)

    ab_re, ab_im, bb_re, bb_im = ssm_discretise(lw['ssm_a_re'], lw['ssm_a_im'], lw['ssm_log_dt'],
                                                lw['ssm_b_re'], lw['ssm_b_im'])
    y_s, h_re, h_im = ssm_scan(u.reshape(bsz, L, SSM_GROUPS, SSM_GROUP), h_re, h_im,
                               ab_re, ab_im, bb_re, bb_im, lw['ssm_c_re'], lw['ssm_c_im'])
    y_s = (y_s.reshape(bsz, L, SSM_WIDTH) + lw['ssm_d'].astype(F32) * u.astype(F32)).astype(x.dtype)
    gl = jax.nn.gelu(y_s) @ lw['w_glu']
    br_a = gl[..., :D_MODEL] * jax.nn.sigmoid(gl[..., D_MODEL:]) * jax.nn.silu(z_a)

    q = rope(q.reshape(bsz, L, N_HEADS, 2, HEAD_DIM), pos)
    k = rope(k.reshape(bsz, L, N_HEADS, 2, HEAD_DIM), pos)
    v = v.reshape(bsz, L, N_HEADS, V_DIM)
    lam = (jnp.exp(jnp.sum(lw['lambda_q1'].astype(F32) * lw['lambda_k1'].astype(F32)))
           - jnp.exp(jnp.sum(lw['lambda_q2'].astype(F32) * lw['lambda_k2'].astype(F32))) + lam_init)
    o = attend(q, k, v, lam).astype(F32)
    o = o * lax.rsqrt(jnp.mean(o * o, axis=-1, keepdims=True) + SUBLN_EPS) * lw['subln_g'].astype(F32)
    o = (o * (1.0 - lam_init)).astype(x.dtype).reshape(bsz, L, ATTN_WIDTH)
    br_b = (o * jax.nn.silu(z_b)) @ lw['w_attn_proj']

    mix = (jax.nn.sigmoid(g_a) * br_a + jax.nn.sigmoid(g_b) * br_b) @ lw['w_out']
    x = layer_norm(DEEPNORM_ALPHA * x + mix, lw['ln_g'], lw['ln_b'])
    x = x + jax.nn.sigmoid(g_p) * (pe @ lw['w_ple'])
    return x, k, v, h_re, h_im


def setup_inputs(seed: int = 0) -> dict:
    key = jax.random.key(seed)
    ks = jax.random.split(key, 32)
    nrm = jax.random.normal
    n_pages = PAST_LEN // PAGE_SIZE
    used = DEC_BATCH * n_pages
    n_pool = used + used // 4
    page_table = jax.random.permutation(ks[0], n_pool)[:used].astype(jnp.int32).reshape(DEC_BATCH, n_pages)
    a_im = math.pi * jnp.arange(SSM_STATE, dtype=F32)[None, None, :] + 0.02 * nrm(ks[11], (DEPTH, SSM_GROUPS, SSM_STATE))
    return {
        'x_prompt': nrm(ks[1], (BATCH, SEQ, D_MODEL), F32),
        'x_sample': nrm(ks[2], (DEC_BATCH, DEC_SEQ, D_MODEL), F32),
        'cache_k': nrm(ks[3], (DEPTH, n_pool, PAGE_SIZE, N_HEADS, 2, HEAD_DIM), F32),
        'cache_v': nrm(ks[4], (DEPTH, n_pool, PAGE_SIZE, N_HEADS, V_DIM), F32),
        'state_ssm_re': 0.3 * nrm(ks[5], (DEPTH, DEC_BATCH, SSM_GROUPS, SSM_STATE), F32),
        'state_ssm_im': 0.3 * nrm(ks[6], (DEPTH, DEC_BATCH, SSM_GROUPS, SSM_STATE), F32),
        'page_table': page_table,
        'p_prompt': nrm(ks[7], (DEPTH, BATCH, SEQ, PLE_DIM), F32),
        'p_sample': nrm(ks[8], (DEPTH, DEC_BATCH, DEC_SEQ, PLE_DIM), F32),
        'w_in': nrm(ks[9], (DEPTH, D_MODEL, IN_COLS), F32) * D_MODEL ** -0.5,
        'ssm_a_re': -0.5 + 0.02 * nrm(ks[10], (DEPTH, SSM_GROUPS, SSM_STATE), F32),
        'ssm_a_im': a_im,
        'ssm_log_dt': jax.random.uniform(ks[12], (DEPTH, SSM_GROUPS), F32, math.log(1e-3), math.log(1e-1)),
        'ssm_b_re': nrm(ks[13], (DEPTH, SSM_GROUPS, SSM_STATE, SSM_GROUP), F32) * (2 * SSM_GROUP) ** -0.5,
        'ssm_b_im': nrm(ks[14], (DEPTH, SSM_GROUPS, SSM_STATE, SSM_GROUP), F32) * (2 * SSM_GROUP) ** -0.5,
        'ssm_c_re': nrm(ks[15], (DEPTH, SSM_GROUPS, SSM_GROUP, SSM_STATE), F32) * SSM_STATE ** -0.5,
        'ssm_c_im': nrm(ks[16], (DEPTH, SSM_GROUPS, SSM_GROUP, SSM_STATE), F32) * SSM_STATE ** -0.5,
        'ssm_d': nrm(ks[17], (DEPTH, SSM_WIDTH), F32),
        'w_glu': nrm(ks[18], (DEPTH, SSM_WIDTH, 2 * D_MODEL), F32) * SSM_WIDTH ** -0.5 * DEEPNORM_BETA,
        'lambda_q1': 0.1 * nrm(ks[19], (DEPTH, HEAD_DIM), F32),
        'lambda_k1': 0.1 * nrm(ks[20], (DEPTH, HEAD_DIM), F32),
        'lambda_q2': 0.1 * nrm(ks[21], (DEPTH, HEAD_DIM), F32),
        'lambda_k2': 0.1 * nrm(ks[22], (DEPTH, HEAD_DIM), F32),
        'subln_g': 1.0 + 0.05 * nrm(ks[23], (DEPTH, V_DIM), F32),
        'w_attn_proj': nrm(ks[24], (DEPTH, ATTN_WIDTH, D_MODEL), F32) * ATTN_WIDTH ** -0.5 * DEEPNORM_BETA,
        'w_out': nrm(ks[25], (DEPTH, D_MODEL, D_MODEL), F32) * D_MODEL ** -0.5 * DEEPNORM_BETA,
        'ln_g': 1.0 + 0.05 * nrm(ks[26], (DEPTH, D_MODEL), F32),
        'ln_b': 0.02 * nrm(ks[27], (DEPTH, D_MODEL), F32),
        'w_ple': nrm(ks[28], (DEPTH, PLE_DIM, D_MODEL), F32) * PLE_DIM ** -0.5,
    }


def reference(x_prompt, x_sample, cache_k, cache_v, state_ssm_re, state_ssm_im, page_table,
              p_prompt, p_sample, w_in, ssm_a_re, ssm_a_im, ssm_log_dt, ssm_b_re, ssm_b_im,
              ssm_c_re, ssm_c_im, ssm_d, w_glu, lambda_q1, lambda_k1, lambda_q2, lambda_k2,
              subln_g, w_attn_proj, w_out, ln_g, ln_b, w_ple):
    bsz, seq = x_prompt.shape[0], x_prompt.shape[1]
    dbsz, dseq = x_sample.shape[0], x_sample.shape[1]
    past_len = page_table.shape[1] * cache_k.shape[2]
    pos_prompt = jnp.arange(seq)
    pos_sample = past_len + jnp.arange(dseq)
    xp, xs = x_prompt, x_sample
    kp_l, vp_l, hpr_l, hpi_l, ks_l, vs_l, hsr_l, hsi_l = [], [], [], [], [], [], [], []
    for l in range(DEPTH):
        lw = dict(w_in=w_in[l], ssm_a_re=ssm_a_re[l], ssm_a_im=ssm_a_im[l], ssm_log_dt=ssm_log_dt[l],
                  ssm_b_re=ssm_b_re[l], ssm_b_im=ssm_b_im[l], ssm_c_re=ssm_c_re[l], ssm_c_im=ssm_c_im[l],
                  ssm_d=ssm_d[l], w_glu=w_glu[l], lambda_q1=lambda_q1[l], lambda_k1=lambda_k1[l],
                  lambda_q2=lambda_q2[l], lambda_k2=lambda_k2[l], subln_g=subln_g[l],
                  w_attn_proj=w_attn_proj[l], w_out=w_out[l], ln_g=ln_g[l], ln_b=ln_b[l], w_ple=w_ple[l])
        lam_init = 0.8 - 0.6 * math.exp(-0.3 * l)
        h0 = jnp.zeros((bsz, SSM_GROUPS, SSM_STATE), F32)
        xp, kp, vp, hpr, hpi = trunk_layer(xp, p_prompt[l], pos_prompt, h0, h0,
                                           diff_attn_prompt, lw, lam_init)
        k_past = cache_k[l][page_table].reshape(dbsz, past_len, N_HEADS, 2, HEAD_DIM)
        v_past = cache_v[l][page_table].reshape(dbsz, past_len, N_HEADS, V_DIM)
        attend_s = functools.partial(diff_attn_sample, k_past=k_past, v_past=v_past)
        xs, ksn, vsn, hsr, hsi = trunk_layer(xs, p_sample[l], pos_sample, state_ssm_re[l], state_ssm_im[l],
                                             attend_s, lw, lam_init)
        kp_l.append(kp); vp_l.append(vp); hpr_l.append(hpr); hpi_l.append(hpi)
        ks_l.append(ksn); vs_l.append(vsn); hsr_l.append(hsr); hsi_l.append(hsi)
    return (xp, xs,
            jnp.stack(kp_l), jnp.stack(vp_l), jnp.stack(hpr_l), jnp.stack(hpi_l),
            jnp.stack(ks_l), jnp.stack(vs_l), jnp.stack(hsr_l), jnp.stack(hsi_l))
```

```python
import functools
import math

import jax
import jax.numpy as jnp
from jax import lax
from jax.experimental import pallas as pl
from jax.experimental.pallas import tpu as pltpu

F32 = jnp.float32
BF16 = jnp.bfloat16

_V7X_VMEM_LIMIT_BYTES = 56 * 1024 * 1024
_LANES = 128

SSM_GROUP = 16
SSM_STATE = 64
SSM_CHUNK = 128
N_HEADS = 8
HEAD_DIM = 64
V_DIM = 2 * HEAD_DIM
ROPE_THETA = 10000.0
LN_EPS = 1e-5
SUBLN_EPS = 1e-5
N_SECTIONS = 9
SEC_U, SEC_ZA, SEC_Q, SEC_K, SEC_V, SEC_ZB, SEC_GA, SEC_GB, SEC_GP = range(N_SECTIONS)
PAGES_PER_STEP = 4


def _params(*sem):
    return pltpu.CompilerParams(dimension_semantics=sem, vmem_limit_bytes=_V7X_VMEM_LIMIT_BYTES)


def _pick_tile(n, pref):
    t = min(n, pref)
    while n % t:
        t //= 2
    return t


def _mm_kernel(x_ref, w_ref, o_ref):
    o_ref[...] = jnp.dot(x_ref[...].astype(BF16), w_ref[...], preferred_element_type=F32)


def _matmul(x, w, tm_pref=1024, tn_pref=1024):
    m, k = x.shape
    n = w.shape[1]
    tm, tn = _pick_tile(m, tm_pref), _pick_tile(n, tn_pref)
    return pl.pallas_call(
        _mm_kernel,
        out_shape=jax.ShapeDtypeStruct((m, n), F32),
        grid=(m // tm, n // tn),
        in_specs=[pl.BlockSpec((tm, k), lambda i, j: (i, 0)),
                  pl.BlockSpec((k, tn), lambda i, j: (0, j))],
        out_specs=pl.BlockSpec((tm, tn), lambda i, j: (i, j)),
        compiler_params=_params("parallel", "arbitrary"),
        name="in_proj",
    )(x, w)


def _rope_kernel(q_ref, k_ref, v_ref, cos_ref, sin_ref, qb_ref, kf_ref, kb_ref, vb_ref, *, width):
    reps = width // _LANES
    cos = jnp.tile(cos_ref[...], (1, reps))
    sin = jnp.tile(sin_ref[...], (1, reps))
    lane = lax.broadcasted_iota(jnp.int32, cos.shape, 1)
    first_half = (lane % HEAD_DIM) < (HEAD_DIM // 2)
    half = HEAD_DIM // 2

    def rot(x):
        partner = jnp.where(first_half, pltpu.roll(x, width - half, 1), pltpu.roll(x, half, 1))
        return x * cos + partner * sin

    q = rot(q_ref[...])
    k = rot(k_ref[...])
    qb_ref[...] = (q * (HEAD_DIM ** -0.5)).astype(BF16)
    kf_ref[...] = k
    kb_ref[...] = k.astype(BF16)
    vb_ref[...] = v_ref[...].astype(BF16)


def _rope_split(proj, cos, sin, d):
    m = proj.shape[0]
    tm = _pick_tile(m, 512)
    sec = lambda s: pl.BlockSpec((tm, d), lambda i, s=s: (i, s))
    tab = pl.BlockSpec((tm, _LANES), lambda i: (i, 0))
    out = pl.BlockSpec((tm, d), lambda i: (i, 0))
    return pl.pallas_call(
        functools.partial(_rope_kernel, width=d),
        out_shape=(jax.ShapeDtypeStruct((m, d), BF16), jax.ShapeDtypeStruct((m, d), F32),
                   jax.ShapeDtypeStruct((m, d), BF16), jax.ShapeDtypeStruct((m, d), BF16)),
        grid=(m // tm,),
        in_specs=[sec(SEC_Q), sec(SEC_K), sec(SEC_V), tab, tab],
        out_specs=(out, out, out, out),
        compiler_params=_params("parallel"),
        name="rope_split",
    )(proj, proj, proj, cos, sin)


def _rope_tables(pos):
    half = HEAD_DIM // 2
    inv = ROPE_THETA ** (-jnp.arange(half, dtype=F32) / half)
    ang = pos.astype(F32)[:, None] * inv[None, :]
    cos, sin = jnp.cos(ang), jnp.sin(ang)
    reps = _LANES // HEAD_DIM
    cos_t = jnp.tile(jnp.concatenate([cos, cos], axis=-1), (1, reps))
    sin_t = jnp.tile(jnp.concatenate([-sin, sin], axis=-1), (1, reps))
    return cos_t, sin_t


def _lambda_value(lq1_ref, lk1_ref, lq2_ref, lk2_ref, lam_init):
    s1 = jnp.sum(lq1_ref[...] * lk1_ref[...], axis=-1, keepdims=True)
    s2 = jnp.sum(lq2_ref[...] * lk2_ref[...], axis=-1, keepdims=True)
    return jnp.exp(s1) - jnp.exp(s2) + lam_init


def _attn_prompt_kernel(lq1_ref, lk1_ref, lq2_ref, lk2_ref, g_ref, q_ref, k_ref, v_ref, o_ref,
                        *, tq, lam_init):
    qi = pl.program_id(1)
    q = q_ref[...]
    lane = lax.broadcasted_iota(jnp.int32, q.shape, 1)
    zero = jnp.zeros_like(q)
    q1 = jnp.where(lane < HEAD_DIM, q, zero)
    q2 = jnp.where(lane >= HEAD_DIM, q, zero)
    nt = (((1,), (1,)), ((), ()))

    def update(carry, qm, kblk, vblk, mask):
        m, l, acc = carry
        s = lax.dot_general(qm, kblk, nt, preferred_element_type=F32)
        if mask is not None:
            s = jnp.where(mask, s, -jnp.inf)
        m_new = jnp.maximum(m, jnp.max(s, axis=-1, keepdims=True))
        alpha = jnp.exp(m - m_new)
        p = jnp.exp(s - m_new)
        l = alpha * l + jnp.sum(p, axis=-1, keepdims=True)
        acc = alpha * acc + jnp.dot(p.astype(BF16), vblk, preferred_element_type=F32)
        return m_new, l, acc

    def block(ki, carry, mask):
        start = pl.multiple_of(ki * tq, tq)
        kblk = k_ref[pl.ds(start, tq), :]
        vblk = v_ref[pl.ds(start, tq), :]
        c1, c2 = carry
        return update(c1, q1, kblk, vblk, mask), update(c2, q2, kblk, vblk, mask)

    init = (jnp.full((tq, 1), -jnp.inf, F32), jnp.zeros((tq, 1), F32), jnp.zeros((tq, V_DIM), F32))
    carry = lax.fori_loop(0, qi, lambda ki, c: block(ki, c, None), (init, init))
    row = lax.broadcasted_iota(jnp.int32, (tq, tq), 0)
    col = lax.broadcasted_iota(jnp.int32, (tq, tq), 1)
    (m1, l1, a1), (m2, l2, a2) = block(qi, carry, col <= row)

    lam = _lambda_value(lq1_ref, lk1_ref, lq2_ref, lk2_ref, lam_init)
    o = a1 / l1 - lam * (a2 / l2)
    o = o * lax.rsqrt(jnp.mean(o * o, axis=-1, keepdims=True) + SUBLN_EPS) * g_ref[...]
    o_ref[...] = o * (1.0 - lam_init)


def _attn_prompt(qb, kb, vb, lams, subln_g, lam_init):
    seq, d = qb.shape
    tq = _pick_tile(seq, 512)
    small = pl.BlockSpec((1, HEAD_DIM), lambda h, i: (0, 0))
    return pl.pallas_call(
        functools.partial(_attn_prompt_kernel, tq=tq, lam_init=lam_init),
        out_shape=jax.ShapeDtypeStruct((seq, d), F32),
        grid=(d // V_DIM, seq // tq),
        in_specs=[small, small, small, small,
                  pl.BlockSpec((1, V_DIM), lambda h, i: (0, 0)),
                  pl.BlockSpec((tq, V_DIM), lambda h, i: (i, h)),
                  pl.BlockSpec((seq, V_DIM), lambda h, i: (0, h)),
                  pl.BlockSpec((seq, V_DIM), lambda h, i: (0, h))],
        out_specs=pl.BlockSpec((tq, V_DIM), lambda h, i: (i, h)),
        compiler_params=_params("parallel", "arbitrary"),
        name="attn_prompt",
    )(*lams, subln_g, qb, kb, vb)


def _attn_decode_kernel(pt_ref, lq1_ref, lk1_ref, lq2_ref, lk2_ref, g_ref, q_ref, kn_ref, vn_ref, *rest,
                        d, lam_init):
    del pt_ref
    pp = PAGES_PER_STEP
    k_refs, v_refs = rest[:pp], rest[pp:2 * pp]
    o_ref, m_scr, l_scr, acc_scr = rest[2 * pp:]
    j = pl.program_id(1)
    nrow = 2 * N_HEADS
    row = lax.broadcasted_iota(jnp.int32, (nrow, d), 0)
    lane = lax.broadcasted_iota(jnp.int32, (nrow, d), 1)
    head, half = row % N_HEADS, row // N_HEADS
    qsel = (lane // HEAD_DIM) == (2 * head + half)
    qbd = jnp.where(qsel, jnp.broadcast_to(q_ref[...], (nrow, d)), 0.0)
    qbd_b = qbd.astype(BF16)
    nt = (((1,), (1,)), ((), ()))

    @pl.when(j == 0)
    def _():
        m_scr[...] = jnp.full(m_scr.shape, -jnp.inf, F32)
        l_scr[...] = jnp.zeros(l_scr.shape, F32)
        acc_scr[...] = jnp.zeros(acc_scr.shape, F32)

    m, l, acc = m_scr[...], l_scr[...], acc_scr[...]
    for r in range(pp):
        s = lax.dot_general(qbd_b, k_refs[r][...].astype(BF16), nt, preferred_element_type=F32)
        m_new = jnp.maximum(m, jnp.max(s, axis=-1, keepdims=True))
        alpha = jnp.exp(m - m_new)
        p = jnp.exp(s - m_new)
        l = alpha * l + jnp.sum(p, axis=-1, keepdims=True)
        acc = alpha * acc + jnp.dot(p.astype(BF16), v_refs[r][...].astype(BF16), preferred_element_type=F32)
        m = m_new
    m_scr[...], l_scr[...], acc_scr[...] = m, l, acc

    @pl.when(j == pl.num_programs(1) - 1)
    def _():
        s_new = jnp.sum(qbd * kn_ref[...], axis=-1, keepdims=True)
        m_f = jnp.maximum(m, s_new)
        alpha = jnp.exp(m - m_f)
        p_new = jnp.exp(s_new - m_f)
        l_f = alpha * l + p_new
        o_maps = (alpha * acc + p_new * vn_ref[...]) / l_f
        lam = _lambda_value(lq1_ref, lk1_ref, lq2_ref, lk2_ref, lam_init)
        o = o_maps[:N_HEADS] - lam * o_maps[N_HEADS:]
        head_row = lax.broadcasted_iota(jnp.int32, (N_HEADS, d), 0)
        head_lane = lax.broadcasted_iota(jnp.int32, (N_HEADS, d), 1) // V_DIM
        o = jnp.where(head_lane == head_row, o, 0.0)
        ms = jnp.sum(o * o, axis=-1, keepdims=True) * (1.0 / V_DIM)
        o = o * lax.rsqrt(ms + SUBLN_EPS)
        o = jnp.sum(o, axis=0, keepdims=True) * jnp.tile(g_ref[...], (1, d // V_DIM))
        o_ref[...] = o * (1.0 - lam_init)


def _attn_decode(q, k_new, v_new, cache_k, cache_v, page_table, lams, subln_g, lam_init):
    b, d = q.shape
    n_pages = page_table.shape[1]
    page = cache_k.shape[1]
    pp = PAGES_PER_STEP
    assert n_pages % pp == 0
    nrow = 2 * N_HEADS

    def page_spec(r):
        return pl.BlockSpec((None, page, d), lambda i, j, pt, r=r: (pt[i * n_pages + j * pp + r], 0, 0))

    small = pl.BlockSpec((1, HEAD_DIM), lambda i, j, pt: (0, 0))
    tok = pl.BlockSpec((None, 1, d), lambda i, j, pt: (i, 0, 0))
    grid_spec = pltpu.PrefetchScalarGridSpec(
        num_scalar_prefetch=1,
        grid=(b, n_pages // pp),
        in_specs=[small, small, small, small, pl.BlockSpec((1, V_DIM), lambda i, j, pt: (0, 0)),
                  tok, tok, tok] + [page_spec(r) for r in range(pp)] * 2,
        out_specs=tok,
        scratch_shapes=[pltpu.VMEM((nrow, 1), F32), pltpu.VMEM((nrow, 1), F32), pltpu.VMEM((nrow, d), F32)],
    )
    tok3 = lambda a: a.reshape(b, 1, d)
    out = pl.pallas_call(
        functools.partial(_attn_decode_kernel, d=d, lam_init=lam_init),
        out_shape=jax.ShapeDtypeStruct((b, 1, d), F32),
        grid_spec=grid_spec,
        compiler_params=_params("parallel", "arbitrary"),
        name="attn_decode",
    )(page_table.reshape(-1), *lams, subln_g, tok3(q), tok3(k_new), tok3(v_new),
      *([cache_k] * pp), *([cache_v] * pp))
    return out.reshape(b, d)


def _ssm_disc_kernel(are_ref, aim_ref, ldt_ref, pwr_ref, pwi_ref, cor_ref, coi_ref, *, n_pow):
    a_re, a_im = are_ref[...], aim_ref[...]
    dt = jnp.exp(ldt_ref[...])
    mag = jnp.exp(a_re * dt)
    ab_re = mag * jnp.cos(a_im * dt)
    ab_im = mag * jnp.sin(a_im * dt)
    den = a_re * a_re + a_im * a_im
    n_re = ab_re - 1.0
    cor_ref[...] = (n_re * a_re + ab_im * a_im) / den
    coi_ref[...] = (ab_im * a_re - n_re * a_im) / den

    def body(t, carry):
        cr, ci = carry
        pwr_ref[t] = cr
        pwi_ref[t] = ci
        return cr * ab_re - ci * ab_im, cr * ab_im + ci * ab_re

    lax.fori_loop(0, n_pow, body, (jnp.ones_like(a_re), jnp.zeros_like(a_re)))


def _ssm_discretise(a_re, a_im, log_dt, n_pow):
    g, p = a_re.shape
    return pl.pallas_call(
        functools.partial(_ssm_disc_kernel, n_pow=n_pow),
        out_shape=(jax.ShapeDtypeStruct((n_pow, g, p), F32), jax.ShapeDtypeStruct((n_pow, g, p), F32),
                   jax.ShapeDtypeStruct((g, p), F32), jax.ShapeDtypeStruct((g, p), F32)),
        name="ssm_disc",
    )(a_re, a_im, log_dt.reshape(g, 1))


def _pad_lanes(x):
    return jnp.pad(x, [(0, 0)] * (x.ndim - 1) + [(0, _LANES - x.shape[-1])])


def _ssm_prompt_kernel(u_ref, cor_ref, coi_ref, btr_ref, bti_ref, cr_ref, ci_ref, ctr_ref, cti_ref,
                       pwtr_ref, pwti_ref, pwrr_ref, pwri_ref, pw1r_ref, pw1i_ref, atr_ref, ati_ref,
                       y_ref, hr_ref, hi_ref, kt_scr, bbr_scr, bbi_scr, rhs_scr, wcr_scr, wci_scr,
                       *, nc, chunk):
    t = chunk
    c = SSM_GROUP
    hi_prec = lax.Precision.HIGHEST
    co_r, co_i = cor_ref[...], coi_ref[...]
    bb_r = co_r * btr_ref[...] - co_i * bti_ref[...]
    bb_i = co_r * bti_ref[...] + co_i * btr_ref[...]
    bbr_scr[...] = bb_r
    bbi_scr[...] = bb_i
    ct_r = jnp.concatenate([cr_ref[...]] * c, axis=0)
    ct_i = jnp.concatenate([ci_ref[...]] * c, axis=0)
    rep = lambda x: jnp.concatenate([jnp.broadcast_to(x[i:i + 1], (c, x.shape[1])) for i in range(c)], axis=0)
    br_r, br_i = rep(bb_r), rep(bb_i)
    m_re = ct_r * br_r - ct_i * br_i
    m_im = ct_r * br_i + ct_i * br_r
    kt_scr[...] = (jnp.dot(m_re, pwtr_ref[...], precision=hi_prec, preferred_element_type=F32)
                   - jnp.dot(m_im, pwti_ref[...], precision=hi_prec, preferred_element_type=F32))

    row = lax.broadcasted_iota(jnp.int32, (t, t), 0)
    col = lax.broadcasted_iota(jnp.int32, (t, t), 1)
    causal = col >= row
    pwr_r, pwr_i = pwrr_ref[...], pwri_ref[...]

    def build(cp, carry):
        r0 = pl.multiple_of(cp * t, t)
        for cc in range(c):
            k = kt_scr[pl.ds(cp * c + cc, 1), :]
            blk = pltpu.roll(jnp.broadcast_to(k, (t, t)), 0, 1, stride=1, stride_axis=0)
            rhs_scr[pl.ds(r0, t), cc * t:(cc + 1) * t] = jnp.where(causal, blk, 0.0).astype(BF16)
        b_r = bbr_scr[pl.ds(cp, 1), :]
        b_i = bbi_scr[pl.ds(cp, 1), :]
        rhs_scr[pl.ds(r0, t), c * t:(c + 1) * t] = (pwr_r * b_r - pwr_i * b_i).astype(BF16)
        rhs_scr[pl.ds(r0, t), (c + 1) * t:(c + 2) * t] = (pwr_r * b_i + pwr_i * b_r).astype(BF16)
        return carry

    lax.fori_loop(0, c, build, 0)

    pw1_r, pw1_i = pw1r_ref[...], pw1i_ref[...]
    for cc in range(c):
        c_r = ctr_ref[:, cc:cc + 1]
        c_i = cti_ref[:, cc:cc + 1]
        wcr_scr[:, cc * t:(cc + 1) * t] = (c_r * pw1_r - c_i * pw1_i).astype(BF16)
        wci_scr[:, cc * t:(cc + 1) * t] = (-(c_r * pw1_i + c_i * pw1_r)).astype(BF16)

    res = jnp.dot(u_ref[...], rhs_scr[...], preferred_element_type=F32)
    s_r = res[:, c * t:(c + 1) * t]
    s_i = res[:, (c + 1) * t:(c + 2) * t]
    a_r = jnp.broadcast_to(atr_ref[...], s_r.shape)
    a_i = jnp.broadcast_to(ati_ref[...], s_r.shape)
    rown = lax.broadcasted_iota(jnp.int32, s_r.shape, 0)
    sh = 1
    while sh < nc:
        keep = rown >= sh
        p_r = jnp.where(keep, pltpu.roll(s_r, sh, 0), 0.0)
        p_i = jnp.where(keep, pltpu.roll(s_i, sh, 0), 0.0)
        s_r, s_i = s_r + a_r * p_r - a_i * p_i, s_i + a_r * p_i + a_i * p_r
        a_r, a_i = a_r * a_r - a_i * a_i, 2.0 * a_r * a_i
        sh *= 2
    hr_ref[...] = s_r[nc - 1:nc]
    hi_ref[...] = s_i[nc - 1:nc]
    if nc > 1:
        h_r = jnp.where(rown >= 1, pltpu.roll(s_r, 1, 0), 0.0)
        h_i = jnp.where(rown >= 1, pltpu.roll(s_i, 1, 0), 0.0)
        y2 = (jnp.dot(h_r.astype(BF16), wcr_scr[...], preferred_element_type=F32)
              + jnp.dot(h_i.astype(BF16), wci_scr[...], preferred_element_type=F32))
        y_ref[...] = res[:, :c * t] + y2
    else:
        y_ref[...] = res[:, :c * t]


def _ssm_prompt(u, pw_r, pw_i, co_r, co_i, b_r, b_i, c_r, c_i):
    seq, width = u.shape
    g, p = co_r.shape
    c, t = SSM_GROUP, SSM_CHUNK
    assert seq % t == 0
    nc = seq // t
    u3 = u.reshape(nc, t, g, c).transpose(2, 0, 3, 1).reshape(g, nc, c * t)
    tl = lambda x: _pad_lanes(jnp.swapaxes(x, 0, 1))
    ts = lambda x: jnp.pad(jnp.transpose(x, (1, 2, 0)), ((0, 0), (0, _LANES - p), (0, 0)))
    pad3 = lambda x: _pad_lanes(x)
    ins = [
        u3,
        pad3(co_r[:, None, :]), pad3(co_i[:, None, :]),
        pad3(jnp.swapaxes(b_r, 1, 2)), pad3(jnp.swapaxes(b_i, 1, 2)),
        pad3(c_r), pad3(c_i),
        jnp.pad(jnp.swapaxes(c_r, 1, 2), ((0, 0), (0, _LANES - p), (0, 0))),
        jnp.pad(jnp.swapaxes(c_i, 1, 2), ((0, 0), (0, _LANES - p), (0, 0))),
        ts(pw_r[:t]), ts(pw_i[:t]),
        tl(pw_r[:t][::-1]), tl(pw_i[:t][::-1]),
        ts(pw_r[1:t + 1]), ts(pw_i[1:t + 1]),
        pad3(pw_r[t][:, None, :]), pad3(pw_i[t][:, None, :]),
    ]
    spec = lambda a: pl.BlockSpec((None,) + a.shape[1:], lambda i: (i, 0, 0))
    n_rhs = (c + 2) * t
    y3, h_r, h_i = pl.pallas_call(
        functools.partial(_ssm_prompt_kernel, nc=nc, chunk=t),
        out_shape=(jax.ShapeDtypeStruct((g, nc, c * t), F32),
                   jax.ShapeDtypeStruct((g, 1, _LANES), F32), jax.ShapeDtypeStruct((g, 1, _LANES), F32)),
        grid=(g,),
        in_specs=[spec(a) for a in ins],
        out_specs=(pl.BlockSpec((None, nc, c * t), lambda i: (i, 0, 0)),
                   pl.BlockSpec((None, 1, _LANES), lambda i: (i, 0, 0)),
                   pl.BlockSpec((None, 1, _LANES), lambda i: (i, 0, 0))),
        scratch_shapes=[pltpu.VMEM((c * c, t), F32), pltpu.VMEM((c, _LANES), F32), pltpu.VMEM((c, _LANES), F32),
                        pltpu.VMEM((c * t, n_rhs), BF16),
                        pltpu.VMEM((_LANES, c * t), BF16), pltpu.VMEM((_LANES, c * t), BF16)],
        compiler_params=_params("parallel"),
        name="ssm_prompt",
    )(*ins)
    y = y3.reshape(g, nc, c, t).transpose(1, 3, 0, 2).reshape(seq, width)
    return y, h_r[:, 0, :p], h_i[:, 0, :p]


def _ssm_step_kernel(u_ref, hr_ref, hi_ref, abr_ref, abi_ref, cor_ref, coi_ref, btr_ref, bti_ref,
                     cr_ref, ci_ref, y_ref, xr_ref, xi_ref):
    hi_prec = lax.Precision.HIGHEST
    co_r, co_i = cor_ref[...], coi_ref[...]
    bb_r = co_r * btr_ref[...] - co_i * bti_ref[...]
    bb_i = co_r * bti_ref[...] + co_i * btr_ref[...]
    u = u_ref[...]
    bu_r = jnp.dot(u, bb_r, precision=hi_prec, preferred_element_type=F32)
    bu_i = jnp.dot(u, bb_i, precision=hi_prec, preferred_element_type=F32)
    a_r, a_i = abr_ref[...], abi_ref[...]
    h_r, h_i = hr_ref[...], hi_ref[...]
    x_r = a_r * h_r - a_i * h_i + bu_r
    x_i = a_r * h_i + a_i * h_r + bu_i
    xr_ref[...] = x_r
    xi_ref[...] = x_i
    nt = (((1,), (1,)), ((), ()))
    y_ref[...] = (lax.dot_general(x_r, cr_ref[...], nt, precision=hi_prec, preferred_element_type=F32)
                  - lax.dot_general(x_i, ci_ref[...], nt, precision=hi_prec, preferred_element_type=F32))


def _ssm_step(u, h_r, h_i, ab_r, ab_i, co_r, co_i, b_r, b_i, c_r, c_i):
    bsz, width = u.shape
    g, p = co_r.shape
    c = SSM_GROUP
    gm = lambda x: jnp.swapaxes(x, 0, 1)
    ins = [gm(u.reshape(bsz, g, c)), gm(h_r), gm(h_i),
           ab_r[:, None, :], ab_i[:, None, :], co_r[:, None, :], co_i[:, None, :],
           jnp.swapaxes(b_r, 1, 2), jnp.swapaxes(b_i, 1, 2), c_r, c_i]
    spec = lambda a: pl.BlockSpec((None,) + a.shape[1:], lambda i: (i, 0, 0))
    y, x_r, x_i = pl.pallas_call(
        _ssm_step_kernel,
        out_shape=(jax.ShapeDtypeStruct((g, bsz, c), F32), jax.ShapeDtypeStruct((g, bsz, p), F32),
                   jax.ShapeDtypeStruct((g, bsz, p), F32)),
        grid=(g,),
        in_specs=[spec(a) for a in ins],
        out_specs=(pl.BlockSpec((None, bsz, c), lambda i: (i, 0, 0)),
                   pl.BlockSpec((None, bsz, p), lambda i: (i, 0, 0)),
                   pl.BlockSpec((None, bsz, p), lambda i: (i, 0, 0))),
        compiler_params=_params("parallel"),
        name="ssm_step",
    )(*ins)
    return gm(y).reshape(bsz, width), gm(x_r), gm(x_i)


def _tail_kernel(x_ref, pe_ref, u_ref, za_ref, zb_ref, ga_ref, gb_ref, gp_ref, ys_ref, on_ref,
                 d_ref, wglu_ref, wap_ref, wout_ref, wple_ref, lng_ref, lnb_ref, o_ref, *, alpha):
    dm = x_ref.shape[-1]
    y_s = ys_ref[...] + d_ref[...] * u_ref[...].astype(F32)
    gl = jnp.dot(jax.nn.gelu(y_s).astype(BF16), wglu_ref[...], preferred_element_type=F32)
    br_a = gl[:, :dm] * jax.nn.sigmoid(gl[:, dm:]) * jax.nn.silu(za_ref[...])
    br_b = jnp.dot((on_ref[...] * jax.nn.silu(zb_ref[...])).astype(BF16), wap_ref[...],
                   preferred_element_type=F32)
    mixin = jax.nn.sigmoid(ga_ref[...]) * br_a + jax.nn.sigmoid(gb_ref[...]) * br_b
    mix = jnp.dot(mixin.astype(BF16), wout_ref[...], preferred_element_type=F32)
    h = alpha * x_ref[...] + mix
    mu = jnp.mean(h, axis=-1, keepdims=True)
    hc = h - mu
    var = jnp.mean(hc * hc, axis=-1, keepdims=True)
    xn = hc * lax.rsqrt(var + LN_EPS) * lng_ref[...] + lnb_ref[...]
    ple = jnp.dot(pe_ref[...].astype(BF16), wple_ref[...], preferred_element_type=F32)
    o_ref[...] = xn + jax.nn.sigmoid(gp_ref[...]) * ple


def _tail(x, pe, proj, y_ssm, o_n, ssm_d, w_glu, w_ap, w_out, w_ple, ln_g, ln_b, alpha):
    m, dm = x.shape
    tm = _pick_tile(m, 256)
    rows = lambda w: pl.BlockSpec((tm, w), lambda i: (i, 0))
    sec = lambda s: pl.BlockSpec((tm, dm), lambda i, s=s: (i, s))
    whole = lambda a: pl.BlockSpec(a.shape, lambda i: (0, 0))
    vec = pl.BlockSpec((1, dm), lambda i: (0, 0))
    return pl.pallas_call(
        functools.partial(_tail_kernel, alpha=alpha),
        out_shape=jax.ShapeDtypeStruct((m, dm), F32),
        grid=(m // tm,),
        in_specs=[rows(dm), rows(pe.shape[1]), sec(SEC_U), sec(SEC_ZA), sec(SEC_ZB), sec(SEC_GA),
                  sec(SEC_GB), sec(SEC_GP), rows(dm), rows(dm), vec,
                  whole(w_glu), whole(w_ap), whole(w_out), whole(w_ple), vec, vec],
        out_specs=rows(dm),
        compiler_params=_params("parallel"),
        name="tail",
    )(x, pe, proj, proj, proj, proj, proj, proj, y_ssm, o_n,
      ssm_d.reshape(1, dm), w_glu, w_ap, w_out, w_ple, ln_g.reshape(1, dm), ln_b.reshape(1, dm))


def kernel(x_prompt, x_sample, cache_k, cache_v, state_ssm_re, state_ssm_im, page_table, p_prompt, p_sample,
           w_in, ssm_a_re, ssm_a_im, ssm_log_dt, ssm_b_re, ssm_b_im, ssm_c_re, ssm_c_im, ssm_d, w_glu,
           lambda_q1, lambda_k1, lambda_q2, lambda_k2, subln_g, w_attn_proj, w_out, ln_g, ln_b, w_ple):
    bsz, seq, dm = x_prompt.shape
    dbsz, dseq, _ = x_sample.shape
    depth = w_in.shape[0]
    assert bsz == 1 and dseq == 1, "kernels are specialised to one prompt sequence and one decode token"
    n_pool, page = cache_k.shape[1], cache_k.shape[2]
    past_len = page_table.shape[1] * page
    alpha = (2 * depth) ** 0.25
    cos_p, sin_p = _rope_tables(jnp.arange(seq))
    cos_s, sin_s = _rope_tables(jnp.full((dbsz,), past_len))

    xp = x_prompt.reshape(seq, dm)
    xs = x_sample.reshape(dbsz, dm)
    outs = [[] for _ in range(8)]
    for l in range(depth):
        lam_init = 0.8 - 0.6 * math.exp(-0.3 * l)
        w_in_b, w_glu_b, w_ap_b = w_in[l].astype(BF16), w_glu[l].astype(BF16), w_attn_proj[l].astype(BF16)
        w_out_b, w_ple_b = w_out[l].astype(BF16), w_ple[l].astype(BF16)
        lams = [a[l].reshape(1, HEAD_DIM) for a in (lambda_q1, lambda_k1, lambda_q2, lambda_k2)]
        g_sub = subln_g[l].reshape(1, V_DIM)
        pw_r, pw_i, co_r, co_i = _ssm_discretise(ssm_a_re[l], ssm_a_im[l], ssm_log_dt[l], SSM_CHUNK + 1)
        ssm_w = (co_r, co_i, ssm_b_re[l], ssm_b_im[l], ssm_c_re[l], ssm_c_im[l])

        proj = _matmul(xp, w_in_b)
        qb, kf, kb, vb = _rope_split(proj, cos_p, sin_p, dm)
        y_ssm, hp_r, hp_i = _ssm_prompt(proj[:, :dm].astype(BF16), pw_r, pw_i, *ssm_w)
        o_n = _attn_prompt(qb, kb, vb, lams, g_sub, lam_init)
        xp = _tail(xp, p_prompt[l].reshape(seq, -1), proj, y_ssm, o_n, ssm_d[l], w_glu_b, w_ap_b, w_out_b,
                   w_ple_b, ln_g[l], ln_b[l], alpha)
        outs[0].append(kf.reshape(bsz, seq, N_HEADS, 2, HEAD_DIM))
        outs[1].append(proj[:, SEC_V * dm:(SEC_V + 1) * dm].reshape(bsz, seq, N_HEADS, V_DIM))
        outs[2].append(hp_r[None])
        outs[3].append(hp_i[None])

        proj_s = _matmul(xs, w_in_b)
        qb_s, kf_s, _, _ = _rope_split(proj_s, cos_s, sin_s, dm)
        v_s = proj_s[:, SEC_V * dm:(SEC_V + 1) * dm]
        y_ssm_s, hs_r, hs_i = _ssm_step(proj_s[:, :dm], state_ssm_re[l], state_ssm_im[l], pw_r[1], pw_i[1], *ssm_w)
        o_n_s = _attn_decode(qb_s.astype(F32), kf_s, v_s, cache_k[l].reshape(n_pool, page, dm),
                             cache_v[l].reshape(n_pool, page, dm), page_table, lams, g_sub, lam_init)
        xs = _tail(xs, p_sample[l].reshape(dbsz, -1), proj_s, y_ssm_s, o_n_s, ssm_d[l], w_glu_b, w_ap_b,
                   w_out_b, w_ple_b, ln_g[l], ln_b[l], alpha)
        outs[4].append(kf_s.reshape(dbsz, dseq, N_HEADS, 2, HEAD_DIM))
        outs[5].append(v_s.reshape(dbsz, dseq, N_HEADS, V_DIM))
        outs[6].append(hs_r)
        outs[7].append(hs_i)

    return (xp.reshape(bsz, seq, dm), xs.reshape(dbsz, dseq, dm)) + tuple(jnp.stack(o) for o in outs)
```
